```python
import jax, jax.numpy as jnp
from jax import lax
import numpy as np

D_MODEL = 1024
BATCH = 8
SEQ = 8192
DEPTH = 2

RET_HEADS = 4
RET_HEAD_DIM = D_MODEL // 8
D_RET = RET_HEADS * RET_HEAD_DIM
RET_CHUNK = 128
ROPE_BASE = 10000.0
POOL_WINDOWS = (2, 4, 8, 16)
POOL_GROUPS = 4
POOL_GROUP_DIM = D_MODEL // 8
D_POOL = POOL_GROUPS * POOL_GROUP_DIM
POOL_OUT_GROUP_DIM = D_MODEL // POOL_GROUPS
D_CONV = D_MODEL // 2
CONV_WIDTH = 31
D_FF = 2816
N_BRANCHES = 3
EPS = 1e-6
D_IN = 4 * D_RET + D_POOL + 2 * D_CONV + N_BRANCHES * D_MODEL

kernel_name = "hybrid_retention_pool_conformer_macaron"


def rmsnorm(x, g):
    xf = x.astype(jnp.float32)
    y = xf * lax.rsqrt(jnp.mean(xf * xf, axis=-1, keepdims=True) + EPS)
    return (y * g.astype(jnp.float32)).astype(x.dtype)


def layernorm(x, g, b):
    xf = x.astype(jnp.float32)
    mu = jnp.mean(xf, axis=-1, keepdims=True)
    var = jnp.mean(jnp.square(xf - mu), axis=-1, keepdims=True)
    y = (xf - mu) * lax.rsqrt(var + EPS)
    return (y * g.astype(jnp.float32) + b.astype(jnp.float32)).astype(x.dtype)


def swiglu_ffn(x, w_in, w_out):
    gate, up = jnp.split(x @ w_in, 2, axis=-1)
    return (jax.nn.silu(gate) * up) @ w_out


def rotary(x, cos, sin):
    half = x.shape[-1] // 2
    x1, x2 = x[..., :half], x[..., half:]
    return jnp.concatenate([x1 * cos - x2 * sin, x2 * cos + x1 * sin], axis=-1)


def retention(q, k, v, positions):
    in_dtype = q.dtype
    B, S, _ = q.shape
    H, dh, C = RET_HEADS, RET_HEAD_DIM, RET_CHUNK
    N = S // C
    q = q.astype(jnp.float32).reshape(B, S, H, dh)
    k = k.astype(jnp.float32).reshape(B, S, H, dh)
    v = v.astype(jnp.float32).reshape(B, S, H, dh)
    inv_freq = ROPE_BASE ** (-jnp.arange(dh // 2, dtype=jnp.float32) / (dh // 2))
    ang = positions.astype(jnp.float32)[..., None] * inv_freq
    cos = jnp.cos(ang)[:, :, None, :]
    sin = jnp.sin(ang)[:, :, None, :]
    q = rotary(q, cos, sin)
    k = rotary(k, cos, sin) * (dh ** -0.5)
    qc = q.reshape(B, N, C, H, dh)
    kc = k.reshape(B, N, C, H, dh)
    vc = v.reshape(B, N, C, H, dh)
    log_gamma = jnp.log1p(-(2.0 ** (-5.0 - jnp.arange(H, dtype=jnp.float32))))
    idx = jnp.arange(C, dtype=jnp.float32)
    diff = idx[:, None] - idx[None, :]
    decay_in = jnp.where(diff >= 0, jnp.exp(log_gamma[:, None, None] * jnp.maximum(diff, 0.0)), 0.0)
    scores = jnp.einsum('bnihd,bnjhd->bnhij', qc, kc) * decay_in
    inner = jnp.einsum('bnhij,bnjhe->bnihe', scores, vc)
    kv_decay = jnp.exp(log_gamma[:, None] * (C - 1 - idx))
    chunk_kv = jnp.einsum('bnjhd,hj,bnjhe->nbhde', kc, kv_decay, vc)
    chunk_decay = jnp.exp(log_gamma * C)[None, :, None, None]

    def step(state, kv):
        return state * chunk_decay + kv, state

    _, prev_states = lax.scan(step, jnp.zeros_like(chunk_kv[0]), chunk_kv)
    q_decay = jnp.exp(log_gamma[:, None] * (idx + 1.0))
    cross = jnp.einsum('bnihd,hi,nbhde->bnihe', qc, q_decay, prev_states)
    out = (inner + cross).reshape(B, S, H, dh)
    out = out * lax.rsqrt(jnp.mean(out * out, axis=-1, keepdims=True) + EPS)
    return out.reshape(B, S, D_RET).astype(in_dtype)


def pool_mixer(p, w_pool, pool_scale):
    B, S, _ = p.shape
    G = POOL_GROUP_DIM
    pf = p.astype(jnp.float32)
    csum = jnp.pad(jnp.cumsum(pf, axis=1), ((0, 0), (1, 0), (0, 0)))
    t = jnp.arange(S, dtype=jnp.float32)
    outs = []
    for gi, w in enumerate(POOL_WINDOWS):
        c = csum[:, :, gi * G:(gi + 1) * G]
        upper = c[:, 1:]
        lower = jnp.concatenate([jnp.zeros((B, w - 1, G), jnp.float32), c[:, :S - w + 1]], axis=1)
        count = jnp.minimum(t + 1.0, float(w))[None, :, None]
        pooled = (upper - lower) / count - pf[:, :, gi * G:(gi + 1) * G]
        outs.append(jnp.einsum('bsg,go->bso', pooled.astype(p.dtype), w_pool[gi]))
    return jnp.concatenate(outs, axis=-1) * pool_scale


def conv_module(c_in, conv_w, conv_b, ln_g, ln_b, w_conv_out):
    a, b = jnp.split(c_in, 2, axis=-1)
    u = a * jax.nn.sigmoid(b)
    u_pad = jnp.pad(u, ((0, 0), (CONV_WIDTH - 1, 0), (0, 0)))
    y = lax.conv_general_dilated(u_pad, conv_w[:, None, :], window_strides=(1,), padding='VALID',
                                 dimension_numbers=('NWC', 'WIO', 'NWC'),
                                 feature_group_count=D_CONV) + conv_b
    y = jax.nn.silu(layernorm(y, ln_g, ln_b))
    return y @ w_conv_out


def setup_inputs(seed: int = 0) -> dict:
    key = jax.random.key(seed)
    ks = jax.random.split(key, 24)
    f32 = jnp.float32
    L, D = DEPTH, D_MODEL

    def nrm(k, shape, scale):
        return jax.random.normal(k, shape, f32) * scale

    def gain(k, shape):
        return 1.0 + 0.05 * jax.random.normal(k, shape, f32)

    return {
        "x": jax.random.normal(ks[0], (BATCH, SEQ, D), f32),
        "positions": jnp.broadcast_to(jnp.arange(SEQ, dtype=jnp.int32)[None, :], (BATCH, SEQ)),
        "norm_ffn1": gain(ks[1], (L, D)),
        "w_ffn1_in": nrm(ks[2], (L, D, 2 * D_FF), D ** -0.5),
        "w_ffn1_out": nrm(ks[3], (L, D_FF, D), D_FF ** -0.5),
        "norm_mix": gain(ks[4], (L, D)),
        "w_in": nrm(ks[5], (L, D, D_IN), D ** -0.5),
        "gate_bias": nrm(ks[6], (L, N_BRANCHES * D), 0.02),
        "w_ret_out": nrm(ks[7], (L, D_RET, D), D_RET ** -0.5),
        "w_pool": nrm(ks[8], (L, POOL_GROUPS, POOL_GROUP_DIM, POOL_OUT_GROUP_DIM), POOL_GROUP_DIM ** -0.5),
        "pool_scale": gain(ks[9], (L, D)),
        "conv_w": nrm(ks[10], (L, CONV_WIDTH, D_CONV), CONV_WIDTH ** -0.5),
        "conv_b": nrm(ks[11], (L, D_CONV), 0.02),
        "conv_ln_g": gain(ks[12], (L, D_CONV)),
        "conv_ln_b": nrm(ks[13], (L, D_CONV), 0.02),
        "w_conv_out": nrm(ks[14], (L, D_CONV, D), D_CONV ** -0.5),
        "w_out": nrm(ks[15], (L, D, D), D ** -0.5),
        "norm_ffn2": gain(ks[16], (L, D)),
        "w_ffn2_in": nrm(ks[17], (L, D, 2 * D_FF), D ** -0.5),
        "w_ffn2_out": nrm(ks[18], (L, D_FF, D), D_FF ** -0.5),
        "norm_final": gain(ks[19], (D,)),
    }


def reference(x, positions, norm_ffn1, w_ffn1_in, w_ffn1_out, norm_mix, w_in, gate_bias,
              w_ret_out, w_pool, pool_scale, conv_w, conv_b, conv_ln_g, conv_ln_b, w_conv_out,
              w_out, norm_ffn2, w_ffn2_in, w_ffn2_out, norm_final):
    B, S, D = x.shape
    split_points = np.cumsum([D_RET, D_RET, D_RET, D_RET, D_POOL, 2 * D_CONV]).tolist()
    for l in range(DEPTH):
        x = x + 0.5 * swiglu_ffn(rmsnorm(x, norm_ffn1[l]), w_ffn1_in[l], w_ffn1_out[l])
        u = rmsnorm(x, norm_mix[l])
        proj = u @ w_in[l]
        q, k, v, g, p, c, gates = jnp.split(proj, split_points, axis=-1)
        y_ret = (retention(q, k, v, positions) * jax.nn.silu(g)) @ w_ret_out[l]
        y_pool = pool_mixer(p, w_pool[l], pool_scale[l])
        y_conv = conv_module(c, conv_w[l], conv_b[l], conv_ln_g[l], conv_ln_b[l], w_conv_out[l])
        gates = jax.nn.sigmoid(gates + gate_bias[l]).reshape(B, S, N_BRANCHES, D)
        merged = gates[:, :, 0] * y_ret + gates[:, :, 1] * y_pool + gates[:, :, 2] * y_conv
        x = x + merged @ w_out[l]
        x = x + 0.5 * swiglu_ffn(rmsnorm(x, norm_ffn2[l]), w_ffn2_in[l], w_ffn2_out[l])
    return rmsnorm(x, norm_final)
```

```python
import functools

import jax
import jax.numpy as jnp
from jax import lax
from jax.experimental import pallas as pl
from jax.experimental.pallas import tpu as pltpu

EPS = 1e-6
RET_HEADS = 4
RET_CHUNK = 128
ROPE_BASE = 10000.0
POOL_WINDOWS = (2, 4, 8, 16)
CONV_WIDTH = 31

V7X_LANES = 128
V7X_SUBLANES = 8
V7X_MXU_DIM = 256
V7X_VMEM_LIMIT_BYTES = 56 * 1024 * 1024

FFN_TOKENS = 512
MIX_TOKENS = 512
POOL_HIST = 16
CONV_HIST = 32
CONV_ROWS = 64

_F32 = jnp.float32
_BF16 = jnp.bfloat16


def _dot(a, b):
    return jnp.dot(a, b, preferred_element_type=_F32)


def _dot_nt(a, b):
    return lax.dot_general(a, b, (((1,), (1,)), ((), ())), preferred_element_type=_F32)


def _dot_tn(a, b):
    return lax.dot_general(a, b, (((0,), (0,)), ((), ())), preferred_element_type=_F32)


def _rmsnorm(x, g):
    return x * lax.rsqrt(jnp.mean(x * x, axis=-1, keepdims=True) + EPS) * g


def _silu(x):
    return x * jax.nn.sigmoid(x)


def _resident(shape):
    zeros = (0,) * len(shape)
    return pl.BlockSpec(shape, lambda *_: zeros, pipeline_mode=pl.Buffered(1))


def _ffn_chunks(d_ff):
    step = 4 * V7X_MXU_DIM
    return [(a, min(a + step, d_ff)) for a in range(0, d_ff, step)]


def _ffn_kernel(x_ref, g_ref, w_in_ref, w_out_ref, gf_ref, o_ref, *, d_ff, final_norm):
    x = x_ref[...]
    u = _rmsnorm(x, g_ref[...]).astype(_BF16)
    acc = None
    for a, b in _ffn_chunks(d_ff):
        gate = _dot(u, w_in_ref[:, a:b])
        up = _dot(u, w_in_ref[:, d_ff + a:d_ff + b])
        act = (_silu(gate) * up).astype(_BF16)
        part = _dot(act, w_out_ref[a:b, :])
        acc = part if acc is None else acc + part
    y = x + 0.5 * acc
    if final_norm:
        y = _rmsnorm(y, gf_ref[...])
    o_ref[...] = y


def _ffn(x2d, g, w_in, w_out, gf, *, final_norm):
    t, d = x2d.shape
    d_ff = w_out.shape[0]
    assert t % FFN_TOKENS == 0 and d_ff % V7X_MXU_DIM == 0
    tile = pl.BlockSpec((FFN_TOKENS, d), lambda i: (i, 0))
    return pl.pallas_call(
        functools.partial(_ffn_kernel, d_ff=d_ff, final_norm=final_norm),
        grid=(t // FFN_TOKENS,),
        in_specs=[tile, _resident((1, d)), _resident(w_in.shape), _resident(w_out.shape), _resident((1, d))],
        out_specs=tile,
        out_shape=jax.ShapeDtypeStruct((t, d), _F32),
        compiler_params=pltpu.CompilerParams(
            dimension_semantics=("arbitrary",), vmem_limit_bytes=V7X_VMEM_LIMIT_BYTES),
        name="ffn_final" if final_norm else "ffn",
    )(x2d, g.reshape(1, d), w_in, w_out, gf.reshape(1, d))


def _mixer_kernel(x_ref, pos_ref, g_ref, w_in_ref, gate_b_ref, w_ret_ref, w_pool_ref, pool_scale_ref,
                  conv_w_ref, conv_b_ref, ln_g_ref, ln_b_ref, w_conv_ref, w_out_ref,
                  inv_freq_ref, sin_sign_ref, decay_in_ref, q_decay_ref, kv_decay_ref, chunk_decay_ref,
                  o_ref, state_ref, ret_ref, pool_ref, conv_ref, convout_ref, *, dims):
    d, d_ret, d_pool, d_conv = dims
    ts = x_ref.shape[0]
    heads, chunk = RET_HEADS, RET_CHUNK
    dh = d_ret // heads
    pool_g = d_pool // len(POOL_WINDOWS)
    s_idx = pl.program_id(1)

    @pl.when(s_idx == 0)
    def _():
        state_ref[...] = jnp.zeros_like(state_ref)
        pool_ref[0:POOL_HIST, :] = jnp.zeros((POOL_HIST, d_pool), _F32)
        conv_ref[0:CONV_HIST, :] = jnp.zeros((CONV_HIST, d_conv), _F32)

    x = x_ref[...]
    u = _rmsnorm(x, g_ref[...]).astype(_BF16)

    o_q, o_k, o_v, o_g = 0, d_ret, 2 * d_ret, 3 * d_ret
    o_p = 4 * d_ret
    o_c = o_p + d_pool
    o_gates = o_c + 2 * d_conv

    qkv = _dot(u, w_in_ref[:, o_q:o_g])
    ang = pos_ref[...] * inv_freq_ref[...]
    cos2 = jnp.cos(ang)
    sin2 = jnp.sin(ang) * sin_sign_ref[...]
    scale = dh ** -0.5
    for n in range(ts // chunk):
        rows = slice(n * chunk, (n + 1) * chunk)
        cn, sn = cos2[rows], sin2[rows]
        for h in range(heads):
            qh = qkv[rows, o_q + h * dh:o_q + (h + 1) * dh]
            kh = qkv[rows, o_k + h * dh:o_k + (h + 1) * dh]
            vh = qkv[rows, o_v + h * dh:o_v + (h + 1) * dh].astype(_BF16)
            qh = qh * cn + pltpu.roll(qh, dh // 2, 1) * sn
            kh = (kh * cn + pltpu.roll(kh, dh // 2, 1) * sn) * scale
            state = state_ref[h]
            scores = _dot_nt(qh.astype(_BF16), kh.astype(_BF16)) * decay_in_ref[h]
            inner = _dot(scores.astype(_BF16), vh)
            cross = _dot((qh * q_decay_ref[h]).astype(_BF16), state.astype(_BF16))
            out = inner + cross
            state_ref[h] = state * chunk_decay_ref[h] + _dot_tn((kh * kv_decay_ref[h]).astype(_BF16), vh)
            out = out * lax.rsqrt(jnp.mean(out * out, axis=-1, keepdims=True) + EPS)
            ret_ref[rows, h * dh:(h + 1) * dh] = out
    g_act = _silu(_dot(u, w_in_ref[:, o_g:o_p]))
    y_ret = _dot((ret_ref[...] * g_act).astype(_BF16), w_ret_ref[...])

    p = _dot(u, w_in_ref[:, o_p:o_c])
    pool_ref[POOL_HIST:POOL_HIST + ts, :] = p
    t_seq = (lax.broadcasted_iota(jnp.int32, (ts, pool_g), 0) + s_idx * ts).astype(_F32)
    y_pool_parts = []
    for gi, w in enumerate(POOL_WINDOWS):
        cols = slice(gi * pool_g, (gi + 1) * pool_g)
        win = p[:, cols]
        for back in range(1, w):
            win = win + pool_ref[POOL_HIST - back:POOL_HIST - back + ts, cols]
        pooled = win / jnp.minimum(t_seq + 1.0, float(w)) - p[:, cols]
        y_pool_parts.append(_dot(pooled.astype(_BF16), w_pool_ref[gi]))
    y_pool = jnp.concatenate(y_pool_parts, axis=-1) * pool_scale_ref[...]
    pool_ref[0:POOL_HIST, :] = pool_ref[ts:ts + POOL_HIST, :]

    c = _dot(u, w_in_ref[:, o_c:o_gates])
    conv_ref[CONV_HIST:CONV_HIST + ts, :] = c[:, :d_conv] * jax.nn.sigmoid(c[:, d_conv:])
    for r0 in range(0, ts, CONV_ROWS):
        acc = jnp.zeros((CONV_ROWS, d_conv), _F32) + conv_b_ref[...]
        for j in range(CONV_WIDTH):
            start = r0 + CONV_HIST - (CONV_WIDTH - 1) + j
            acc = acc + conv_ref[start:start + CONV_ROWS, :] * conv_w_ref[j:j + 1, :]
        convout_ref[r0:r0 + CONV_ROWS, :] = acc
    conv_ref[0:CONV_HIST, :] = conv_ref[ts:ts + CONV_HIST, :]
    yc = convout_ref[...]
    mu = jnp.mean(yc, axis=-1, keepdims=True)
    var = jnp.mean(jnp.square(yc - mu), axis=-1, keepdims=True)
    yc = (yc - mu) * lax.rsqrt(var + EPS) * ln_g_ref[...] + ln_b_ref[...]
    y_conv = _dot(_silu(yc).astype(_BF16), w_conv_ref[...])

    def gate(i):
        cols = slice(o_gates + i * d, o_gates + (i + 1) * d)
        return jax.nn.sigmoid(_dot(u, w_in_ref[:, cols]) + gate_b_ref[:, i * d:(i + 1) * d])

    merged = gate(0) * y_ret + gate(1) * y_pool + gate(2) * y_conv
    o_ref[...] = x + _dot(merged.astype(_BF16), w_out_ref[...])


def _mixer(x, pos, g, w_in, gate_b, w_ret, w_pool, pool_scale, conv_w, conv_b, ln_g, ln_b, w_conv, w_out, consts):
    b, s, d = x.shape
    d_ret, d_conv = w_ret.shape[0], w_conv.shape[0]
    d_pool = w_pool.shape[0] * w_pool.shape[1]
    ts = MIX_TOKENS
    assert s % ts == 0 and ts % RET_CHUNK == 0 and d_ret // RET_HEADS == V7X_LANES
    assert POOL_HIST >= max(POOL_WINDOWS) - 1 and CONV_HIST >= CONV_WIDTH - 1
    tile = pl.BlockSpec((None, ts, d), lambda i, j: (i, j, 0))
    row = lambda a: a.reshape(1, -1)
    small = [row(g), w_in, row(gate_b), w_ret, w_pool, row(pool_scale), conv_w, row(conv_b), row(ln_g), row(ln_b),
             w_conv, w_out, *consts]
    return pl.pallas_call(
        functools.partial(_mixer_kernel, dims=(d, d_ret, d_pool, d_conv)),
        grid=(b, s // ts),
        in_specs=[tile, pl.BlockSpec((None, ts, 1), lambda i, j: (i, j, 0))] + [_resident(a.shape) for a in small],
        out_specs=tile,
        out_shape=jax.ShapeDtypeStruct((b, s, d), _F32),
        scratch_shapes=[
            pltpu.VMEM((RET_HEADS, V7X_LANES, V7X_LANES), _F32),
            pltpu.VMEM((ts, d_ret), _F32),
            pltpu.VMEM((POOL_HIST + ts, d_pool), _F32),
            pltpu.VMEM((CONV_HIST + ts, d_conv), _F32),
            pltpu.VMEM((ts, d_conv), _F32),
        ],
        compiler_params=pltpu.CompilerParams(
            dimension_semantics=("arbitrary", "arbitrary"), vmem_limit_bytes=V7X_VMEM_LIMIT_BYTES),
        name="mixer",
    )(x, pos, *small)


def _retention_consts(dh):
    heads, c = RET_HEADS, RET_CHUNK
    inv_freq = ROPE_BASE ** (-jnp.arange(dh // 2, dtype=_F32) / (dh // 2))
    inv_freq2 = jnp.concatenate([inv_freq, inv_freq]).reshape(1, dh)
    sin_sign = jnp.concatenate([-jnp.ones((dh // 2,), _F32), jnp.ones((dh // 2,), _F32)]).reshape(1, dh)
    log_gamma = jnp.log1p(-(2.0 ** (-5.0 - jnp.arange(heads, dtype=_F32))))
    idx = jnp.arange(c, dtype=_F32)
    diff = idx[:, None] - idx[None, :]
    decay_in = jnp.where(diff >= 0, jnp.exp(log_gamma[:, None, None] * jnp.maximum(diff, 0.0)), 0.0)
    kv_decay = jnp.exp(log_gamma[:, None] * (c - 1 - idx))
    q_decay = jnp.exp(log_gamma[:, None] * (idx + 1.0))
    chunk_decay = jnp.exp(log_gamma * c)
    lanes = lambda a: jnp.broadcast_to(a[:, :, None], (heads, c, dh))
    return [inv_freq2, sin_sign, decay_in, lanes(q_decay), lanes(kv_decay),
            jnp.broadcast_to(chunk_decay[:, None, None], (heads, dh, dh))]


def kernel(x, positions, norm_ffn1, w_ffn1_in, w_ffn1_out, norm_mix, w_in, gate_bias, w_ret_out, w_pool, pool_scale, conv_w, conv_b, conv_ln_g, conv_ln_b, w_conv_out, w_out, norm_ffn2, w_ffn2_in, w_ffn2_out, norm_final):
    b, s, d = x.shape
    depth = w_in.shape[0]
    bf = lambda a: a.astype(_BF16)
    pos = positions.astype(_F32).reshape(b, s, 1)
    consts = _retention_consts(w_ret_out.shape[1] // RET_HEADS)
    for l in range(depth):
        x2d = _ffn(x.reshape(b * s, d), norm_ffn1[l], bf(w_ffn1_in[l]), bf(w_ffn1_out[l]), norm_final,
                   final_norm=False)
        x = _mixer(x2d.reshape(b, s, d), pos, norm_mix[l], bf(w_in[l]), gate_bias[l], bf(w_ret_out[l]),
                   bf(w_pool[l]), pool_scale[l], conv_w[l], conv_b[l], conv_ln_g[l], conv_ln_b[l],
                   bf(w_conv_out[l]), bf(w_out[l]), consts)
        x2d = _ffn(x.reshape(b * s, d), norm_ffn2[l], bf(w_ffn2_in[l]), bf(w_ffn2_out[l]), norm_final,
                   final_norm=(l == depth - 1))
        x = x2d.reshape(b, s, d)
    return x
```

```python
import functools

import jax
import jax.numpy as jnp
from jax import lax
from jax.experimental import pallas as pl
from jax.experimental.pallas import tpu as pltpu

EPS = 1e-6
RET_HEADS = 4
RET_CHUNK = 128
ROPE_BASE = 10000.0
POOL_WINDOWS = (2, 4, 8, 16)
CONV_WIDTH = 31

V7X_LANES = 128
V7X_SUBLANES = 8
V7X_MXU_DIM = 256
V7X_VMEM_LIMIT_BYTES = 56 * 1024 * 1024

FFN_TOKENS = 512
MIX_TOKENS = 512
POOL_HIST = 16
CONV_HIST = 32
SHIFT_STRIDE = 4
SHIFT_ROWS = SHIFT_STRIDE * V7X_SUBLANES

_F32 = jnp.float32
_BF16 = jnp.bfloat16


def _dot(a, b):
    return jnp.dot(a, b, preferred_element_type=_F32)


def _dot_nt(a, b):
    return lax.dot_general(a, b, (((1,), (1,)), ((), ())), preferred_element_type=_F32)


def _dot_tn(a, b):
    return lax.dot_general(a, b, (((0,), (0,)), ((), ())), preferred_element_type=_F32)


def _rmsnorm(x, g):
    return x * lax.rsqrt(jnp.mean(x * x, axis=-1, keepdims=True) + EPS) * g


def _silu(x):
    return x * jax.nn.sigmoid(x)


def _resident(shape):
    zeros = (0,) * len(shape)
    return pl.BlockSpec(shape, lambda *_: zeros, pipeline_mode=pl.Buffered(1))


def _ffn_chunks(d_ff):
    step = 4 * V7X_MXU_DIM
    return [(a, min(a + step, d_ff)) for a in range(0, d_ff, step)]


def _ffn_kernel(x_ref, g_ref, w_in_ref, w_out_ref, gf_ref, o_ref, *, d_ff, final_norm):
    x = x_ref[...]
    u = _rmsnorm(x, g_ref[...]).astype(_BF16)
    acc = None
    for a, b in _ffn_chunks(d_ff):
        gate = _dot(u, w_in_ref[:, a:b])
        up = _dot(u, w_in_ref[:, d_ff + a:d_ff + b])
        act = (_silu(gate) * up).astype(_BF16)
        part = _dot(act, w_out_ref[a:b, :])
        acc = part if acc is None else acc + part
    y = x + 0.5 * acc
    if final_norm:
        y = _rmsnorm(y, gf_ref[...])
    o_ref[...] = y


def _ffn(x2d, g, w_in, w_out, gf, *, final_norm):
    t, d = x2d.shape
    d_ff = w_out.shape[0]
    assert t % FFN_TOKENS == 0 and d_ff % V7X_MXU_DIM == 0
    tile = pl.BlockSpec((FFN_TOKENS, d), lambda i: (i, 0))
    return pl.pallas_call(
        functools.partial(_ffn_kernel, d_ff=d_ff, final_norm=final_norm),
        grid=(t // FFN_TOKENS,),
        in_specs=[tile, _resident((1, d)), _resident(w_in.shape), _resident(w_out.shape), _resident((1, d))],
        out_specs=tile,
        out_shape=jax.ShapeDtypeStruct((t, d), _F32),
        compiler_params=pltpu.CompilerParams(
            dimension_semantics=("arbitrary",), vmem_limit_bytes=V7X_VMEM_LIMIT_BYTES),
        name="ffn_final" if final_norm else "ffn",
    )(x2d, g.reshape(1, d), w_in, w_out, gf.reshape(1, d))


def _strided_rows(ref, g, start):
    return ref.at[g][pl.ds(start, V7X_SUBLANES, stride=SHIFT_STRIDE), :]


def _mixer_kernel(x_ref, pos_ref, g_ref, w_in_ref, gate_b_ref, w_ret_ref, w_pool_ref, pool_scale_ref,
                  conv_w_ref, conv_b_ref, ln_g_ref, ln_b_ref, w_conv_ref, w_out_ref,
                  inv_freq_ref, sin_sign_ref, decay_in_ref, q_decay_ref, kv_decay_ref, chunk_decay_ref,
                  o_ref, state_ref, ret_ref, pool_ref, pooled_ref, conv_ref, convout_ref, *, dims):
    d, d_ret, d_pool, d_conv = dims
    ts = x_ref.shape[0]
    heads, chunk = RET_HEADS, RET_CHUNK
    n_chunks = ts // chunk
    dh = d_ret // heads
    lanes = V7X_LANES
    s_idx = pl.program_id(1)

    @pl.when(s_idx == 0)
    def _():
        state_ref[...] = jnp.zeros_like(state_ref)
        pool_ref[:, 0:POOL_HIST, :] = jnp.zeros((d_pool // lanes, POOL_HIST, lanes), _F32)
        conv_ref[:, 0:CONV_HIST, :] = jnp.zeros((d_conv // lanes, CONV_HIST, lanes), _F32)

    x = x_ref[...]
    u = _rmsnorm(x, g_ref[...]).astype(_BF16)

    o_q, o_k, o_v, o_g = 0, d_ret, 2 * d_ret, 3 * d_ret
    o_p = 4 * d_ret
    o_c = o_p + d_pool
    o_gates = o_c + 2 * d_conv

    c = _dot(u, w_in_ref[:, o_c:o_gates])
    glu = c[:, :d_conv] * jax.nn.sigmoid(c[:, d_conv:])
    for g in range(d_conv // lanes):
        conv_ref[g, CONV_HIST:CONV_HIST + ts, :] = glu[:, g * lanes:(g + 1) * lanes]
    qkv = _dot(u, w_in_ref[:, o_q:o_g])
    p = _dot(u, w_in_ref[:, o_p:o_c])
    for g in range(d_pool // lanes):
        pool_ref[g, POOL_HIST:POOL_HIST + ts, :] = p[:, g * lanes:(g + 1) * lanes]
    g_act = _silu(_dot(u, w_in_ref[:, o_g:o_p]))

    for g in range(d_conv // lanes):
        cols = slice(g * lanes, (g + 1) * lanes)
        taps = [jnp.broadcast_to(conv_w_ref[j:j + 1, cols], (V7X_SUBLANES, lanes)) for j in range(CONV_WIDTH)]
        bias = jnp.broadcast_to(conv_b_ref[:, cols], (V7X_SUBLANES, lanes))
        for r0 in range(0, ts, SHIFT_ROWS):
            base = CONV_HIST + r0
            rows = {s: _strided_rows(conv_ref, g, base + s) for s in range(1 - CONV_WIDTH, SHIFT_STRIDE)}
            for s in range(SHIFT_STRIDE):
                acc = bias
                for j in range(CONV_WIDTH):
                    acc = acc + rows[s - (CONV_WIDTH - 1) + j] * taps[j]
                convout_ref.at[g][pl.ds(r0 + s, V7X_SUBLANES, stride=SHIFT_STRIDE), :] = acc
        conv_ref[g, 0:CONV_HIST, :] = conv_ref[g, ts:ts + CONV_HIST, :]

    ang = pos_ref[...] * inv_freq_ref[...]
    cos2 = jnp.cos(ang)
    sin2 = jnp.sin(ang) * sin_sign_ref[...]
    scale = dh ** -0.5
    scores, kv, q_dec, v_bf = {}, {}, {}, {}
    for h in range(heads):
        qh = qkv[:, o_q + h * dh:o_q + (h + 1) * dh]
        kh = qkv[:, o_k + h * dh:o_k + (h + 1) * dh]
        qh = qh * cos2 + pltpu.roll(qh, dh // 2, 1) * sin2
        kh = (kh * cos2 + pltpu.roll(kh, dh // 2, 1) * sin2) * scale
        vh = qkv[:, o_v + h * dh:o_v + (h + 1) * dh].astype(_BF16)
        for n in range(n_chunks):
            rows = slice(n * chunk, (n + 1) * chunk)
            qn, kn = qh[rows], kh[rows]
            v_bf[n, h] = vh[rows]
            scores[n, h] = (_dot_nt(qn.astype(_BF16), kn.astype(_BF16)) * decay_in_ref[h]).astype(_BF16)
            kv[n, h] = _dot_tn((kn * kv_decay_ref[h]).astype(_BF16), v_bf[n, h])
            q_dec[n, h] = (qn * q_decay_ref[h]).astype(_BF16)

    for g, w in enumerate(POOL_WINDOWS):
        for r0 in range(0, ts, SHIFT_ROWS):
            base = POOL_HIST + r0
            rows = {s: _strided_rows(pool_ref, g, base + s) for s in range(1 - w, SHIFT_STRIDE)}
            for s in range(SHIFT_STRIDE):
                win = rows[s]
                for back in range(1, w):
                    win = win + rows[s - back]
                if r0 < w - 1:
                    t_seq = (lax.broadcasted_iota(jnp.int32, (V7X_SUBLANES, lanes), 0) * SHIFT_STRIDE
                             + (s_idx * ts + r0 + s)).astype(_F32)
                    win = win / jnp.minimum(t_seq + 1.0, float(w))
                else:
                    win = win * (1.0 / w)
                pooled_ref.at[g][pl.ds(r0 + s, V7X_SUBLANES, stride=SHIFT_STRIDE), :] = win - rows[s]
        pool_ref[g, 0:POOL_HIST, :] = pool_ref[g, ts:ts + POOL_HIST, :]

    def gate(i):
        cols = slice(o_gates + i * d, o_gates + (i + 1) * d)
        return jax.nn.sigmoid(_dot(u, w_in_ref[:, cols]) + gate_b_ref[:, i * d:(i + 1) * d])

    gates = [gate(i) for i in range(3)]

    for h in range(heads):
        state = state_ref[h]
        for n in range(n_chunks):
            lhs = jnp.concatenate([scores[n, h], q_dec[n, h]], axis=1)
            rhs = jnp.concatenate([v_bf[n, h], state.astype(_BF16)], axis=0)
            out = _dot(lhs, rhs)
            out = out * lax.rsqrt(jnp.mean(out * out, axis=-1, keepdims=True) + EPS)
            ret_ref[n * chunk:(n + 1) * chunk, h * dh:(h + 1) * dh] = out
            state = state * chunk_decay_ref[h] + kv[n, h]
        state_ref[h] = state

    y_pool = jnp.concatenate(
        [_dot(pooled_ref[g].astype(_BF16), w_pool_ref[g]) for g in range(len(POOL_WINDOWS))],
        axis=-1) * pool_scale_ref[...]

    yc = jnp.concatenate([convout_ref[g] for g in range(d_conv // lanes)], axis=-1)
    mu = jnp.mean(yc, axis=-1, keepdims=True)
    var = jnp.mean(jnp.square(yc - mu), axis=-1, keepdims=True)
    yc = (yc - mu) * lax.rsqrt(var + EPS) * ln_g_ref[...] + ln_b_ref[...]
    y_conv = _dot(_silu(yc).astype(_BF16), w_conv_ref[...])

    y_ret = _dot((ret_ref[...] * g_act).astype(_BF16), w_ret_ref[...])

    merged = gates[0] * y_ret + gates[1] * y_pool + gates[2] * y_conv
    o_ref[...] = x + _dot(merged.astype(_BF16), w_out_ref[...])


def _mixer(x, pos, g, w_in, gate_b, w_ret, w_pool, pool_scale, conv_w, conv_b, ln_g, ln_b, w_conv, w_out, consts):
    b, s, d = x.shape
    d_ret, d_conv = w_ret.shape[0], w_conv.shape[0]
    d_pool = w_pool.shape[0] * w_pool.shape[1]
    ts = MIX_TOKENS
    assert s % ts == 0 and ts % RET_CHUNK == 0 and ts % SHIFT_ROWS == 0
    assert d_ret // RET_HEADS == V7X_LANES and w_pool.shape[1] == V7X_LANES and d_conv % V7X_LANES == 0
    assert POOL_HIST >= max(POOL_WINDOWS) - 1 and CONV_HIST >= CONV_WIDTH - 1
    tile = pl.BlockSpec((None, ts, d), lambda i, j: (i, j, 0))
    row = lambda a: a.reshape(1, -1)
    small = [row(g), w_in, row(gate_b), w_ret, w_pool, row(pool_scale), conv_w, row(conv_b), row(ln_g), row(ln_b),
             w_conv, w_out, *consts]
    return pl.pallas_call(
        functools.partial(_mixer_kernel, dims=(d, d_ret, d_pool, d_conv)),
        grid=(b, s // ts),
        in_specs=[tile, pl.BlockSpec((None, ts, 1), lambda i, j: (i, j, 0))] + [_resident(a.shape) for a in small],
        out_specs=tile,
        out_shape=jax.ShapeDtypeStruct((b, s, d), _F32),
        scratch_shapes=[
            pltpu.VMEM((RET_HEADS, V7X_LANES, V7X_LANES), _F32),
            pltpu.VMEM((ts, d_ret), _F32),
            pltpu.VMEM((d_pool // V7X_LANES, POOL_HIST + ts, V7X_LANES), _F32),
            pltpu.VMEM((d_pool // V7X_LANES, ts, V7X_LANES), _F32),
            pltpu.VMEM((d_conv // V7X_LANES, CONV_HIST + ts, V7X_LANES), _F32),
            pltpu.VMEM((d_conv // V7X_LANES, ts, V7X_LANES), _F32),
        ],
        compiler_params=pltpu.CompilerParams(
            dimension_semantics=("arbitrary", "arbitrary"), vmem_limit_bytes=V7X_VMEM_LIMIT_BYTES),
        name="mixer",
    )(x, pos, *small)


def _retention_consts(dh):
    heads, c = RET_HEADS, RET_CHUNK
    inv_freq = ROPE_BASE ** (-jnp.arange(dh // 2, dtype=_F32) / (dh // 2))
    inv_freq2 = jnp.concatenate([inv_freq, inv_freq]).reshape(1, dh)
    sin_sign = jnp.concatenate([-jnp.ones((dh // 2,), _F32), jnp.ones((dh // 2,), _F32)]).reshape(1, dh)
    log_gamma = jnp.log1p(-(2.0 ** (-5.0 - jnp.arange(heads, dtype=_F32))))
    idx = jnp.arange(c, dtype=_F32)
    diff = idx[:, None] - idx[None, :]
    decay_in = jnp.where(diff >= 0, jnp.exp(log_gamma[:, None, None] * jnp.maximum(diff, 0.0)), 0.0)
    kv_decay = jnp.exp(log_gamma[:, None] * (c - 1 - idx))
    q_decay = jnp.exp(log_gamma[:, None] * (idx + 1.0))
    chunk_decay = jnp.exp(log_gamma * c)
    lanes = lambda a: jnp.broadcast_to(a[:, :, None], (heads, c, dh))
    return [inv_freq2, sin_sign, decay_in, lanes(q_decay), lanes(kv_decay),
            jnp.broadcast_to(chunk_decay[:, None, None], (heads, dh, dh))]


def kernel(x, positions, norm_ffn1, w_ffn1_in, w_ffn1_out, norm_mix, w_in, gate_bias, w_ret_out, w_pool, pool_scale, conv_w, conv_b, conv_ln_g, conv_ln_b, w_conv_out, w_out, norm_ffn2, w_ffn2_in, w_ffn2_out, norm_final):
    b, s, d = x.shape
    depth = w_in.shape[0]
    bf = lambda a: a.astype(_BF16)
    pos = positions.astype(_F32).reshape(b, s, 1)
    consts = _retention_consts(w_ret_out.shape[1] // RET_HEADS)
    for l in range(depth):
        x2d = _ffn(x.reshape(b * s, d), norm_ffn1[l], bf(w_ffn1_in[l]), bf(w_ffn1_out[l]), norm_final,
                   final_norm=False)
        x = _mixer(x2d.reshape(b, s, d), pos, norm_mix[l], bf(w_in[l]), gate_bias[l], bf(w_ret_out[l]),
                   bf(w_pool[l]), pool_scale[l], conv_w[l], conv_b[l], conv_ln_g[l], conv_ln_b[l],
                   bf(w_conv_out[l]), bf(w_out[l]), consts)
        x2d = _ffn(x.reshape(b * s, d), norm_ffn2[l], bf(w_ffn2_in[l]), bf(w_ffn2_out[l]), norm_final,
                   final_norm=(l == depth - 1))
        x = x2d.reshape(b, s, d)
    return x
```
